```python
import jax, jax.numpy as jnp
from jax import lax
import numpy as np

D_MODEL = 1024
BATCH = 2
SEQ = 8192
DEPTH = 2

D_MIX = D_MODEL
N_GROUPS = 4
GROUP_W = D_MIX // N_GROUPS
HEAD_DIM = 64
ROT_DIM = HEAD_DIM // 4
ROPE_THETA = 500000.0
EPS = 1e-6
D_FF = 2816
SG_HEADS = GROUP_W // HEAD_DIM
SG_CHUNK = 128
SC_WIDTH = 3
DSA_HEADS = GROUP_W // HEAD_DIM
KV_LATENT = 128
IDX_HEADS = 4
IDX_DIM = 64
IDX_TOPK = 256
DSA_BLOCK = 128
GDN_HEADS = GROUP_W // HEAD_DIM
GDN_CONV = 4
GDN_CHUNK = 64
MEM_LEN = 256
XA_HEADS = 4
XA_HEAD_DIM = D_MODEL // XA_HEADS

SPLIT_SIZES = (2 * GROUP_W, 3 * GROUP_W, GROUP_W, KV_LATENT, IDX_HEADS * IDX_DIM, IDX_DIM,
               IDX_HEADS, 3 * GROUP_W, GDN_HEADS, GDN_HEADS, GROUP_W)
N_MIX_IN = sum(SPLIT_SIZES)

kernel_name = "hybrid_parallel_group_decoder"


def rmsnorm(x, g):
    xf = x.astype(jnp.float32)
    y = xf * lax.rsqrt(jnp.mean(xf * xf, axis=-1, keepdims=True) + EPS)
    return (y * g.astype(jnp.float32)).astype(x.dtype)


def swiglu(h, wg, wu, wd):
    return (jax.nn.silu(h @ wg) * (h @ wu)) @ wd


def causal_dwconv(x, w):
    K = w.shape[-1]
    S = x.shape[1]
    xp = jnp.pad(x, ((0, 0), (K - 1, 0), (0, 0)))
    return sum(xp[:, j:j + S, :] * w[:, j] for j in range(K))


def partial_rope(x, pos):
    half = ROT_DIM // 2
    inv = ROPE_THETA ** (-jnp.arange(half, dtype=jnp.float32) * (2.0 / ROT_DIM))
    ang = pos.astype(jnp.float32)[:, :, None, None] * inv
    cos, sin = jnp.cos(ang), jnp.sin(ang)
    xf = x.astype(jnp.float32)
    x1, x2, xp = xf[..., :half], xf[..., half:ROT_DIM], xf[..., ROT_DIM:]
    return jnp.concatenate([x1 * cos - x2 * sin, x2 * cos + x1 * sin, xp], -1).astype(x.dtype)


def spatial_gating(uv, norm_g, w_s, b_s):
    Bsz, S, _ = uv.shape
    u, v = jnp.split(jax.nn.gelu(uv), 2, axis=-1)
    v = rmsnorm(v, norm_g)
    n = S // SG_CHUNK
    vc = v.reshape(Bsz, n, SG_CHUNK, SG_HEADS, HEAD_DIM)
    causal = jnp.tril(jnp.ones((SG_CHUNK, SG_CHUNK), bool))
    w = jnp.where(causal, w_s, 0.0).astype(v.dtype)
    mixed = jnp.einsum('hts,bnshd->bnthd', w, vc) + b_s.T.astype(v.dtype)[None, None, :, :, None]
    return u * mixed.reshape(Bsz, S, GROUP_W)


def short_gated_conv(bch, w_conv):
    b, c, h = jnp.split(bch, 3, axis=-1)
    return b * causal_dwconv(c * h, w_conv)


def sparse_indexed_attention(q, ckv, iq, ik, iw, pos, kv_norm, w_uk, w_uv, ik_norm):
    Bsz, S, _ = q.shape
    c = rmsnorm(ckv, kv_norm)
    k = partial_rope((c @ w_uk).reshape(Bsz, S, DSA_HEADS, HEAD_DIM), pos)
    v = (c @ w_uv).reshape(Bsz, S, DSA_HEADS, HEAD_DIM)
    q = partial_rope(q.reshape(Bsz, S, DSA_HEADS, HEAD_DIM), pos)
    iq = partial_rope(iq.reshape(Bsz, S, IDX_HEADS, IDX_DIM), pos)
    ik = partial_rope(rmsnorm(ik, ik_norm)[:, :, None, :], pos)[:, :, 0, :]
    iw = iw.astype(jnp.float32) * (IDX_HEADS ** -0.5)
    topk = min(IDX_TOPK, S // 4)
    nb = S // DSA_BLOCK
    kv = jnp.concatenate([k, v], axis=-1)
    key_idx = jnp.arange(S)

    def to_blocks(t):
        return jnp.swapaxes(t.reshape(Bsz, nb, DSA_BLOCK, *t.shape[2:]), 0, 1)

    def block(args):
        qb, iqb, iwb, start = args
        qpos = start + jnp.arange(DSA_BLOCK)
        s = jnp.einsum('bqhd,bsd->bqhs', iqb, ik, preferred_element_type=jnp.float32) * (IDX_DIM ** -0.5)
        score = jnp.einsum('bqh,bqhs->bqs', iwb, jax.nn.relu(s))
        score = jnp.where(key_idx[None, None, :] <= qpos[None, :, None], score, -jnp.inf)
        _, idx = lax.top_k(score, topk)
        kvg = jax.vmap(lambda a, i: a[i])(kv, idx)
        kg, vg = kvg[..., :HEAD_DIM], kvg[..., HEAD_DIM:]
        logits = jnp.einsum('bqhd,bqkhd->bqhk', qb, kg, preferred_element_type=jnp.float32) * (HEAD_DIM ** -0.5)
        valid = (idx <= qpos[None, :, None])[:, :, None, :]
        p = jax.nn.softmax(jnp.where(valid, logits, -jnp.inf), axis=-1)
        return jnp.einsum('bqhk,bqkhd->bqhd', p.astype(vg.dtype), vg)

    starts = jnp.arange(nb) * DSA_BLOCK
    out = lax.map(block, (to_blocks(q), to_blocks(iq), to_blocks(iw), starts))
    return jnp.swapaxes(out, 0, 1).reshape(Bsz, S, GROUP_W)


def chunk_gated_delta_rule(q, k, v, g, beta):
    Bsz, S, H, dk = q.shape
    dv = v.shape[-1]
    C = GDN_CHUNK
    n = S // C

    def blk(t):
        return jnp.moveaxis(t.reshape(Bsz, n, C, H, -1), 3, 1)

    q, k, v = blk(q), blk(k), blk(v)
    beta = blk(beta[..., None])[..., 0]
    gc = jnp.cumsum(blk(g[..., None])[..., 0], axis=-1)
    incl = jnp.tril(jnp.ones((C, C), bool))
    strict = jnp.tril(jnp.ones((C, C), bool), -1)
    decay = jnp.exp(jnp.where(incl, gc[..., :, None] - gc[..., None, :], -jnp.inf))
    kb = k * beta[..., None]
    a_mat = jnp.where(strict, jnp.einsum('bhnid,bhnjd->bhnij', kb, k) * decay, 0.0)
    lhs = a_mat + jnp.eye(C, dtype=jnp.float32)
    rhs = jnp.concatenate([v * beta[..., None], kb * jnp.exp(gc)[..., None]], axis=-1)
    sol = lax.linalg.triangular_solve(lhs, rhs, left_side=True, lower=True, unit_diagonal=True)
    u, w = sol[..., :dv], sol[..., dv:]
    attn = jnp.where(incl, jnp.einsum('bhnid,bhnjd->bhnij', q, k) * decay, 0.0)

    def step(state, xs):
        q_i, k_i, u_i, w_i, attn_i, gc_i = xs
        v_new = u_i - jnp.einsum('bhck,bhkv->bhcv', w_i, state)
        o_i = (jnp.einsum('bhck,bhkv->bhcv', q_i * jnp.exp(gc_i)[..., None], state)
               + jnp.einsum('bhcs,bhsv->bhcv', attn_i, v_new))
        g_last = gc_i[..., -1:]
        state = (state * jnp.exp(g_last)[..., None]
                 + jnp.einsum('bhck,bhcv->bhkv', k_i * jnp.exp(g_last - gc_i)[..., None], v_new))
        return state, o_i

    xs = tuple(jnp.moveaxis(t, 2, 0) for t in (q, k, u, w, attn, gc))
    state0 = jnp.zeros((Bsz, H, dk, dv), jnp.float32)
    _, o = lax.scan(step, state0, xs)
    return o.transpose(1, 0, 3, 2, 4).reshape(Bsz, S, H, dv)


def gated_deltanet(qkv, a, b, gate, w_conv, a_log, dt_bias, out_norm):
    Bsz, S, _ = qkv.shape
    dtype = qkv.dtype
    qkv = jax.nn.silu(causal_dwconv(qkv, w_conv))
    q, k, v = [t.reshape(Bsz, S, GDN_HEADS, HEAD_DIM).astype(jnp.float32) for t in jnp.split(qkv, 3, axis=-1)]
    q = q * lax.rsqrt(jnp.sum(q * q, axis=-1, keepdims=True) + EPS)
    k = k * lax.rsqrt(jnp.sum(k * k, axis=-1, keepdims=True) + EPS)
    beta = jax.nn.sigmoid(b.astype(jnp.float32))
    g = -jnp.exp(a_log.astype(jnp.float32)) * jax.nn.softplus(a.astype(jnp.float32) + dt_bias.astype(jnp.float32))
    o = chunk_gated_delta_rule(q * (HEAD_DIM ** -0.5), k, v, g, beta)
    o = rmsnorm(o, out_norm) * jax.nn.silu(gate.reshape(Bsz, S, GDN_HEADS, HEAD_DIM).astype(jnp.float32))
    return o.reshape(Bsz, S, GROUP_W).astype(dtype)


def memory_cross_attention(h, m, wq, wk, wv, wo):
    Bsz, S, _ = h.shape
    q = (h @ wq).reshape(Bsz, S, XA_HEADS, XA_HEAD_DIM)
    k = (m @ wk).reshape(Bsz, -1, XA_HEADS, XA_HEAD_DIM)
    v = (m @ wv).reshape(Bsz, -1, XA_HEADS, XA_HEAD_DIM)
    logits = jnp.einsum('bqhd,bkhd->bhqk', q, k, preferred_element_type=jnp.float32) * (XA_HEAD_DIM ** -0.5)
    p = jax.nn.softmax(logits, axis=-1)
    o = jnp.einsum('bhqk,bkhd->bqhd', p.astype(v.dtype), v).reshape(Bsz, S, D_MODEL)
    return o @ wo


def setup_inputs(seed: int = 0) -> dict:
    key = jax.random.key(seed)
    ks = iter(list(jax.random.split(key, 48)))
    L = DEPTH

    def nrm(shape, fan_in):
        return jax.random.normal(next(ks), shape, jnp.float32) * (fan_in ** -0.5)

    def gain(shape):
        return 1.0 + 0.05 * jax.random.normal(next(ks), shape, jnp.float32)

    x = jax.random.normal(next(ks), (BATCH, SEQ, D_MODEL), jnp.float32)
    mem = jax.random.normal(next(ks), (BATCH, MEM_LEN, D_MODEL), jnp.float32)
    offset = jax.random.randint(next(ks), (BATCH, 1), 0, 4096, dtype=jnp.int32)
    positions = offset + jnp.arange(SEQ, dtype=jnp.int32)[None, :]
    d = {"x": x, "mem": mem, "positions": positions}
    d["ffn1_norm"] = gain((L, D_MODEL))
    d["ffn1_wg"] = nrm((L, D_MODEL, D_FF), D_MODEL)
    d["ffn1_wu"] = nrm((L, D_MODEL, D_FF), D_MODEL)
    d["ffn1_wd"] = nrm((L, D_FF, D_MODEL), D_FF)
    d["mix_norm"] = gain((L, D_MODEL))
    d["w_mix_in"] = nrm((L, D_MODEL, N_MIX_IN), D_MODEL)
    d["w_mix_out"] = nrm((L, D_MIX, D_MODEL), D_MIX)
    d["sg_norm"] = gain((L, GROUP_W))
    d["sg_w"] = nrm((L, SG_HEADS, SG_CHUNK, SG_CHUNK), SG_CHUNK)
    d["sg_b"] = 1.0 + 0.1 * jax.random.normal(next(ks), (L, SG_HEADS, SG_CHUNK), jnp.float32)
    d["sc_conv"] = nrm((L, GROUP_W, SC_WIDTH), SC_WIDTH)
    d["kv_norm"] = gain((L, KV_LATENT))
    d["w_uk"] = nrm((L, KV_LATENT, GROUP_W), KV_LATENT)
    d["w_uv"] = nrm((L, KV_LATENT, GROUP_W), KV_LATENT)
    d["idx_k_norm"] = gain((L, IDX_DIM))
    d["gdn_conv"] = nrm((L, 3 * GROUP_W, GDN_CONV), GDN_CONV)
    d["gdn_a_log"] = jnp.log(jax.random.uniform(next(ks), (L, GDN_HEADS), jnp.float32, 1.0, 16.0))
    dt = jnp.exp(jax.random.uniform(next(ks), (L, GDN_HEADS), jnp.float32, np.log(1e-3), np.log(1e-1)))
    d["gdn_dt_bias"] = dt + jnp.log(-jnp.expm1(-dt))
    d["gdn_out_norm"] = gain((L, HEAD_DIM))
    d["xa_norm"] = gain((L, D_MODEL))
    d["mem_norm"] = gain((L, D_MODEL))
    d["xa_wq"] = nrm((L, D_MODEL, D_MODEL), D_MODEL)
    d["xa_wk"] = nrm((L, D_MODEL, D_MODEL), D_MODEL)
    d["xa_wv"] = nrm((L, D_MODEL, D_MODEL), D_MODEL)
    d["xa_wo"] = nrm((L, D_MODEL, D_MODEL), D_MODEL)
    d["ffn2_norm"] = gain((L, D_MODEL))
    d["ffn2_wg"] = nrm((L, D_MODEL, D_FF), D_MODEL)
    d["ffn2_wu"] = nrm((L, D_MODEL, D_FF), D_MODEL)
    d["ffn2_wd"] = nrm((L, D_FF, D_MODEL), D_FF)
    d["final_norm"] = gain((D_MODEL,))
    return d


def reference(x, mem, positions, ffn1_norm, ffn1_wg, ffn1_wu, ffn1_wd, mix_norm, w_mix_in, w_mix_out,
              sg_norm, sg_w, sg_b, sc_conv, kv_norm, w_uk, w_uv, idx_k_norm, gdn_conv, gdn_a_log,
              gdn_dt_bias, gdn_out_norm, xa_norm, mem_norm, xa_wq, xa_wk, xa_wv, xa_wo,
              ffn2_norm, ffn2_wg, ffn2_wu, ffn2_wd, final_norm):
    split_points = [int(p) for p in np.cumsum(SPLIT_SIZES)[:-1]]
    for l in range(DEPTH):
        x = x + 0.5 * swiglu(rmsnorm(x, ffn1_norm[l]), ffn1_wg[l], ffn1_wu[l], ffn1_wd[l])
        h = rmsnorm(x, mix_norm[l])
        (p_sg, p_sc, p_q, p_ckv, p_iq, p_ik, p_iw,
         p_gqkv, p_ga, p_gb, p_gg) = jnp.split(h @ w_mix_in[l], split_points, axis=-1)
        y_a = spatial_gating(p_sg, sg_norm[l], sg_w[l], sg_b[l])
        y_b = short_gated_conv(p_sc, sc_conv[l])
        y_c = sparse_indexed_attention(p_q, p_ckv, p_iq, p_ik, p_iw, positions,
                                       kv_norm[l], w_uk[l], w_uv[l], idx_k_norm[l])
        y_d = gated_deltanet(p_gqkv, p_ga, p_gb, p_gg, gdn_conv[l], gdn_a_log[l],
                             gdn_dt_bias[l], gdn_out_norm[l])
        x = x + jnp.concatenate([y_a, y_b, y_c, y_d], axis=-1) @ w_mix_out[l]
        x = x + memory_cross_attention(rmsnorm(x, xa_norm[l]), rmsnorm(mem, mem_norm[l]),
                                       xa_wq[l], xa_wk[l], xa_wv[l], xa_wo[l])
        x = x + 0.5 * swiglu(rmsnorm(x, ffn2_norm[l]), ffn2_wg[l], ffn2_wu[l], ffn2_wd[l])
    return rmsnorm(x, final_norm)
```

```python
import functools

import jax
import jax.numpy as jnp
import numpy as np
from jax import lax
from jax.experimental import pallas as pl
from jax.experimental.pallas import tpu as pltpu

F32 = jnp.float32
BF16 = jnp.bfloat16

EPS = 1e-6
GROUP_W = 256
HEAD_DIM = 64
N_HEADS = GROUP_W // HEAD_DIM
ROT_DIM = HEAD_DIM // 4
ROPE_THETA = 500000.0
KV_LATENT = 128
IDX_DIM = 64
IDX_TOPK = 256
SG_CHUNK = 128
SC_WIDTH = 3
GDN_CONV = 4
GDN_CHUNK = 64
XA_HEADS = 4

LANES = 128
HALO = 8
VMEM_LIMIT = 56 * 1024 * 1024

SM_IW = IDX_DIM
SM_GA = SM_IW + N_HEADS
SM_GB = SM_GA + N_HEADS

NEG_BIG = -1e30


def _params(*sem):
    return pltpu.CompilerParams(dimension_semantics=sem, vmem_limit_bytes=VMEM_LIMIT)


def _rms(x, g):
    return x * lax.rsqrt(jnp.mean(x * x, axis=-1, keepdims=True) + EPS) * g


def _mm(a, b):
    return jnp.dot(a, b, preferred_element_type=F32)


def _mm_nt(a, b):
    return lax.dot_general(a, b, (((1,), (1,)), ((), ())), preferred_element_type=F32)


def _mm_tn(a, b):
    return lax.dot_general(a, b, (((0,), (0,)), ((), ())), preferred_element_type=F32)


def _split3(x):
    hi = x.astype(BF16)
    r = x - hi.astype(F32)
    mid = r.astype(BF16)
    lo = (r - mid.astype(F32)).astype(BF16)
    return hi, mid, lo


def _mm3(a, b):
    ah = a.astype(BF16)
    al = (a - ah.astype(F32)).astype(BF16)
    bh = b.astype(BF16)
    bl = (b - bh.astype(F32)).astype(BF16)
    return _mm(ah, bh) + (_mm(ah, bl) + _mm(al, bh))


def _head_of_lane(shape, dim):
    return lax.shift_right_logical(lax.broadcasted_iota(jnp.int32, shape, dim), 6)


def _ffn_kernel(x_ref, g_ref, wg_ref, wu_ref, wd_ref, *rest, n_ff, final):
    if final:
        fg_ref, o_ref, h_scr, acc_scr = rest
    else:
        o_ref, h_scr, acc_scr = rest
    j = pl.program_id(1)

    @pl.when(j == 0)
    def _():
        h_scr[...] = _rms(x_ref[...], g_ref[...]).astype(BF16)
        acc_scr[...] = jnp.zeros_like(acc_scr)

    h = h_scr[...]
    a = _mm(h, wg_ref[...])
    b = _mm(h, wu_ref[...])
    t = (a * jax.nn.sigmoid(a)) * b
    acc_scr[...] += _mm(t.astype(BF16), wd_ref[...])

    @pl.when(j == n_ff - 1)
    def _():
        y = x_ref[...] + 0.5 * acc_scr[...]
        if final:
            y = _rms(y, fg_ref[...])
        o_ref[...] = y


def _ffn(x, g, wg, wu, wd, final_g=None, tm=512, tf=1408):
    n, d = x.shape
    ff = wg.shape[1]
    n_ff = ff // tf
    final = final_g is not None
    in_specs = [
        pl.BlockSpec((tm, d), lambda i, j: (i, 0)),
        pl.BlockSpec((1, d), lambda i, j: (0, 0)),
        pl.BlockSpec((d, tf), lambda i, j: (0, j)),
        pl.BlockSpec((d, tf), lambda i, j: (0, j)),
        pl.BlockSpec((tf, d), lambda i, j: (j, 0)),
    ]
    args = [x, g.reshape(1, d), wg.astype(BF16), wu.astype(BF16), wd.astype(BF16)]
    if final:
        in_specs.append(pl.BlockSpec((1, d), lambda i, j: (0, 0)))
        args.append(final_g.reshape(1, d))
    return pl.pallas_call(
        functools.partial(_ffn_kernel, n_ff=n_ff, final=final),
        out_shape=jax.ShapeDtypeStruct((n, d), F32),
        grid=(n // tm, n_ff),
        in_specs=in_specs,
        out_specs=pl.BlockSpec((tm, d), lambda i, j: (i, 0)),
        scratch_shapes=[pltpu.VMEM((tm, d), BF16), pltpu.VMEM((tm, d), F32)],
        compiler_params=_params("parallel", "arbitrary"),
        name="ffn",
    )(*args)


def _proj_kernel(x_ref, g_ref, w_ref, *o_refs, widths):
    h = _rms(x_ref[...], g_ref[...]).astype(BF16)
    off = 0
    for o_ref, wd in zip(o_refs, widths):
        o_ref[...] = _mm(h, w_ref[:, off:off + wd]).astype(o_ref.dtype)
        off += wd


def _norm_proj(x, g, w, widths, dtypes, tm=512):
    n, d = x.shape
    tm = min(tm, n)
    return pl.pallas_call(
        functools.partial(_proj_kernel, widths=widths),
        out_shape=[jax.ShapeDtypeStruct((n, wd), dt) for wd, dt in zip(widths, dtypes)],
        grid=(n // tm,),
        in_specs=[
            pl.BlockSpec((tm, d), lambda i: (i, 0)),
            pl.BlockSpec((1, d), lambda i: (0, 0)),
            pl.BlockSpec(w.shape, lambda i: (0, 0)),
        ],
        out_specs=[pl.BlockSpec((tm, wd), lambda i: (i, 0)) for wd in widths],
        compiler_params=_params("parallel"),
        name="norm_proj",
    )(x, g.reshape(1, d), w.astype(BF16))


def _gelu_tanh(x):
    return 0.5 * x * (1.0 + jnp.tanh(np.sqrt(2.0 / np.pi).astype(np.float32) * (x + 0.044715 * (x * x * x))))


def _sg_kernel(uv_ref, g_ref, w_ref, b_ref, o_ref, *, n_chunks):
    uv = _gelu_tanh(uv_ref[...])
    u = uv[:, :GROUP_W]
    v = _rms(uv[:, GROUP_W:], g_ref[...]).astype(BF16)
    row = lax.broadcasted_iota(jnp.int32, (SG_CHUNK, SG_CHUNK), 0)
    col = lax.broadcasted_iota(jnp.int32, (SG_CHUNK, SG_CHUNK), 1)
    ws = [jnp.where(col <= row, w_ref[h], 0.0).astype(BF16) for h in range(N_HEADS)]
    head = _head_of_lane((SG_CHUNK, GROUP_W), 1)
    bias = b_ref[...]
    for c in range(n_chunks):
        rows = slice(c * SG_CHUNK, (c + 1) * SG_CHUNK)
        vc = v[rows, :]
        mixed = jnp.zeros((SG_CHUNK, GROUP_W), F32)
        for h in range(N_HEADS):
            mixed = jnp.where(head == h, _mm(ws[h], vc), mixed)
        o_ref[rows, :] = (u[rows, :] * (mixed + bias)).astype(o_ref.dtype)


def _spatial_gating(uv, g, w, b, tm=1024):
    n = uv.shape[0]
    bias = jnp.repeat(b.T, HEAD_DIM, axis=1)
    return pl.pallas_call(
        functools.partial(_sg_kernel, n_chunks=tm // SG_CHUNK),
        out_shape=jax.ShapeDtypeStruct((n, GROUP_W), BF16),
        grid=(n // tm,),
        in_specs=[
            pl.BlockSpec((tm, 2 * GROUP_W), lambda i: (i, 0)),
            pl.BlockSpec((1, GROUP_W), lambda i: (0, 0)),
            pl.BlockSpec(w.shape, lambda i: (0, 0, 0)),
            pl.BlockSpec(bias.shape, lambda i: (0, 0)),
        ],
        out_specs=pl.BlockSpec((tm, GROUP_W), lambda i: (i, 0)),
        compiler_params=_params("parallel"),
        name="spatial_gating",
    )(uv, g.reshape(1, GROUP_W), w, bias)


def _causal_conv(scr, cur, halo, w_ref, first, width, ts):
    scr[0:HALO, :] = jnp.where(first, 0.0, halo)
    scr[HALO:HALO + ts, :] = cur
    y = None
    for j in range(width):
        off = HALO - (width - 1) + j
        term = scr[off:off + ts, :] * w_ref[j:j + 1, :]
        y = term if y is None else y + term
    return y


def _sc_kernel(x_ref, hx_ref, w_ref, o_ref, scr, *, tiles_per_seq, ts):
    first = (pl.program_id(0) % tiles_per_seq) == 0
    x = x_ref[...]
    hx = hx_ref[...]
    ch = x[:, GROUP_W:2 * GROUP_W] * x[:, 2 * GROUP_W:]
    hch = hx[:, GROUP_W:2 * GROUP_W] * hx[:, 2 * GROUP_W:]
    y = _causal_conv(scr, ch, hch, w_ref, first, SC_WIDTH, ts)
    o_ref[...] = (x[:, :GROUP_W] * y).astype(o_ref.dtype)


def _short_conv(bch, w, seq, ts=1024):
    n = bch.shape[0]
    ts = min(ts, seq)
    hb = ts // HALO
    return pl.pallas_call(
        functools.partial(_sc_kernel, tiles_per_seq=seq // ts, ts=ts),
        out_shape=jax.ShapeDtypeStruct((n, GROUP_W), BF16),
        grid=(n // ts,),
        in_specs=[
            pl.BlockSpec((ts, 3 * GROUP_W), lambda i: (i, 0)),
            pl.BlockSpec((HALO, 3 * GROUP_W), lambda i: (jnp.maximum(i * hb - 1, 0), 0)),
            pl.BlockSpec((SC_WIDTH, GROUP_W), lambda i: (0, 0)),
        ],
        out_specs=pl.BlockSpec((ts, GROUP_W), lambda i: (i, 0)),
        scratch_shapes=[pltpu.VMEM((ts + HALO, GROUP_W), F32)],
        compiler_params=_params("parallel"),
        name="short_conv",
    )(bch, bch, w.T)


def _rope(x, cos, sin):
    w = x.shape[1]
    lane = lax.broadcasted_iota(jnp.int32, x.shape, 1) & (HEAD_DIM - 1)
    half = ROT_DIM // 2
    swapped = jnp.where(lane < half, pltpu.roll(x, w - half, 1), pltpu.roll(x, half, 1))
    return x * cos + swapped * sin


def _dsa_prep_kernel(q_ref, ckv_ref, iq_ref, sm_ref, cos_ref, sin_ref, kvn_ref, wuk_ref, wuv_ref, ikn_ref,
                     qo_ref, ko_ref, vo_ref, iqo_ref, iko_ref):
    cos1 = cos_ref[...]
    sin1 = sin_ref[...]
    cos2 = jnp.concatenate([cos1, cos1], axis=1)
    sin2 = jnp.concatenate([sin1, sin1], axis=1)
    c = _rms(ckv_ref[...], kvn_ref[...]).astype(BF16)
    ko_ref[...] = _rope(_mm(c, wuk_ref[...]), cos2, sin2).astype(BF16)
    vo_ref[...] = _mm(c, wuv_ref[...]).astype(BF16)
    qo_ref[...] = _rope(q_ref[...], cos2, sin2).astype(BF16)
    iqo_ref[...] = _rope(iq_ref[...], cos2, sin2).astype(BF16)
    sm = sm_ref[...]
    lane = lax.broadcasted_iota(jnp.int32, sm.shape, 1)
    is_ik = lane < IDX_DIM
    ms = jnp.sum(jnp.where(is_ik, sm * sm, 0.0), axis=-1, keepdims=True) * (1.0 / IDX_DIM)
    ik = sm * lax.rsqrt(ms + EPS) * ikn_ref[...]
    ik = jnp.where(is_ik, _rope(ik, cos1, sin1), 0.0)
    ik2 = ik + pltpu.roll(ik, IDX_DIM, 1)
    iko_ref[...] = jnp.concatenate([ik2, ik2], axis=1).astype(BF16)


def _dsa_prep(q, ckv, iq, small, cos, sin, kv_norm, w_uk, w_uv, ik_norm, tm=1024):
    n = q.shape[0]
    ikn = jnp.zeros((1, LANES), F32).at[0, :IDX_DIM].set(ik_norm)
    row = lambda wd: pl.BlockSpec((tm, wd), lambda i: (i, 0))
    full = lambda a: pl.BlockSpec(a.shape, lambda i: (0, 0))
    wuk = w_uk.astype(BF16)
    wuv = w_uv.astype(BF16)
    kvn = kv_norm.reshape(1, KV_LATENT)
    return pl.pallas_call(
        _dsa_prep_kernel,
        out_shape=[jax.ShapeDtypeStruct((n, GROUP_W), BF16)] * 5,
        grid=(n // tm,),
        in_specs=[row(GROUP_W), row(KV_LATENT), row(GROUP_W), row(LANES), row(LANES), row(LANES),
                  full(kvn), full(wuk), full(wuv), full(ikn)],
        out_specs=[row(GROUP_W)] * 5,
        compiler_params=_params("parallel"),
        name="dsa_prep",
    )(q, ckv, iq, small, cos, sin, kvn, wuk, wuv, ikn)


def _key_to_float(key):
    bits = key ^ (lax.shift_right_arithmetic(key, 31) & jnp.int32(0x7FFFFFFF))
    return lax.bitcast_convert_type(bits, F32)


def _dsa_kernel(q_ref, iq_ref, sm_ref, k_ref, v_ref, ik_ref, o_ref, score_scr, *, tq, kb, tiles_per_seq, topk):
    qs = (pl.program_id(0) % tiles_per_seq) * tq
    n_kb = (qs + tq + kb - 1) // kb
    head = _head_of_lane((tq, GROUP_W), 1)
    q = q_ref[...]
    iq = iq_ref[...]
    zero = jnp.zeros_like(q)
    qh = [jnp.where(head == h, q, zero) for h in range(N_HEADS)]
    iqh = [jnp.where(head == h, iq, zero) for h in range(N_HEADS)]
    sm = sm_ref[...]
    iw = [sm[:, SM_IW + h:SM_IW + h + 1] * (N_HEADS ** -0.5) for h in range(N_HEADS)]
    qpos = qs + lax.broadcasted_iota(jnp.int32, (tq, kb), 0)
    kcol = lax.broadcasted_iota(jnp.int32, (tq, kb), 1)

    def score_body(j, carry):
        ikb = ik_ref[pl.ds(pl.multiple_of(j * kb, kb), kb), :]
        score = jnp.zeros((tq, kb), F32)
        for h in range(N_HEADS):
            s = _mm_nt(iqh[h], ikb) * (IDX_DIM ** -0.5)
            score = score + iw[h] * jnp.maximum(s, 0.0)
        score_scr[j] = jnp.where(j * kb + kcol <= qpos, score, -jnp.inf)
        return carry

    lax.fori_loop(0, n_kb, score_body, 0)

    def count(pred):
        def body(j, acc):
            hit = jnp.where(pred(score_scr[j]), 1.0, 0.0)
            part = hit[:, 0:LANES]
            for t in range(1, kb // LANES):
                part = part + hit[:, t * LANES:(t + 1) * LANES]
            return acc + part
        acc = lax.fori_loop(0, n_kb, body, jnp.zeros((tq, LANES), F32))
        return jnp.sum(acc, axis=-1, keepdims=True)

    def count_ge(key):
        cand = _key_to_float(key)
        return count(lambda s: s >= cand)

    int_min = jnp.int32(-2 ** 31)
    thr_key = jnp.where(count_ge(jnp.zeros((tq, 1), jnp.int32)) >= topk, jnp.int32(0), int_min)

    key_neg_inf = jnp.int32(0x807FFFFF - 2 ** 32)

    def bit_body(b, key):
        cand = key + lax.shift_left(jnp.int32(1), 30 - b)
        return jnp.where((count_ge(cand) >= topk) | (cand < key_neg_inf), cand, key)

    thr = _key_to_float(lax.fori_loop(0, 31, bit_body, thr_key))
    need = topk - count(lambda s: s > thr)

    tri_r = lax.broadcasted_iota(jnp.int32, (kb, kb), 0)
    tri_c = lax.broadcasted_iota(jnp.int32, (kb, kb), 1)
    upper = jnp.where(tri_r <= tri_c, 1.0, 0.0).astype(BF16)

    def attn_body(j, carry):
        acc, ms, ls, n_eq = carry
        score = score_scr[j]
        eq = score == thr
        rank = n_eq + _mm(jnp.where(eq, 1.0, 0.0).astype(BF16), upper)
        sel = ((score > thr) | (eq & (rank <= need))) & (j * kb + kcol <= qpos)
        kblk = k_ref[pl.ds(pl.multiple_of(j * kb, kb), kb), :]
        vblk = v_ref[pl.ds(pl.multiple_of(j * kb, kb), kb), :]
        new_ms, new_ls = [], []
        for h in range(N_HEADS):
            logit = jnp.where(sel, _mm_nt(qh[h], kblk) * (HEAD_DIM ** -0.5), NEG_BIG)
            m_new = jnp.maximum(ms[h], jnp.max(logit, axis=-1, keepdims=True))
            alpha = jnp.exp(ms[h] - m_new)
            p = jnp.where(sel, jnp.exp(logit - m_new), 0.0)
            new_ls.append(alpha * ls[h] + jnp.sum(p, axis=-1, keepdims=True))
            new_ms.append(m_new)
            acc = jnp.where(head == h, alpha * acc + _mm(p.astype(BF16), vblk), acc)
        return acc, tuple(new_ms), tuple(new_ls), rank[:, kb - 1:kb]

    col0 = jnp.zeros((tq, 1), F32)
    init = (jnp.zeros((tq, GROUP_W), F32), (col0 + NEG_BIG,) * N_HEADS, (col0,) * N_HEADS, col0)
    acc, _, ls, _ = lax.fori_loop(0, n_kb, attn_body, init)
    inv = jnp.zeros((tq, GROUP_W), F32)
    for h in range(N_HEADS):
        inv = jnp.where(head == h, 1.0 / ls[h], inv)
    o_ref[...] = (acc * inv).astype(o_ref.dtype)


def _dsa(q, iq, small, k, v, ik, seq, tq=128, kb=512):
    n = q.shape[0]
    kb = min(kb, seq)
    tiles = seq // tq
    topk = min(IDX_TOPK, seq // 4)
    row = lambda wd: pl.BlockSpec((tq, wd), lambda i: (i, 0))
    per_seq = pl.BlockSpec((seq, GROUP_W), lambda i: (i // tiles, 0))
    return pl.pallas_call(
        functools.partial(_dsa_kernel, tq=tq, kb=kb, tiles_per_seq=tiles, topk=topk),
        out_shape=jax.ShapeDtypeStruct((n, GROUP_W), BF16),
        grid=(n // tq,),
        in_specs=[row(GROUP_W), row(GROUP_W), row(LANES), per_seq, per_seq, per_seq],
        out_specs=row(GROUP_W),
        scratch_shapes=[pltpu.VMEM((seq // kb, tq, kb), F32)],
        compiler_params=_params("arbitrary"),
        name="dsa",
    )(q, iq, small, k, v, ik)


def _expand_rows(x, headmask):
    return jnp.where(headmask, jnp.concatenate([x] * N_HEADS, axis=0), 0.0)


def _col_rows(cols):
    return jnp.concatenate([jnp.broadcast_to(c, (c.shape[0], GROUP_W)) for c in cols], axis=0)


def _gdn_kernel(x_ref, hx_ref, sm_ref, gate_ref, cw_ref, alog_ref, dtb_ref, onorm_ref, o_ref,
                scr, state, *, tiles_per_seq, ts):
    c_len = GDN_CHUNK
    e_len = N_HEADS * c_len
    first = (pl.program_id(0) % tiles_per_seq) == 0

    @pl.when(first)
    def _():
        state[...] = jnp.zeros_like(state)

    y = _causal_conv(scr, x_ref[...], hx_ref[...], cw_ref, first, GDN_CONV, ts)
    y = y * jax.nn.sigmoid(y)
    grp_r = _head_of_lane((GROUP_W, GROUP_W), 0)
    grp_c = _head_of_lane((GROUP_W, GROUP_W), 1)
    same_head = grp_r == grp_c
    ones_bd = jnp.where(same_head, 1.0, 0.0).astype(BF16)

    def head_sumsq(t):
        hi = t.astype(BF16)
        lo = (t - hi.astype(F32)).astype(BF16)
        return _mm(hi, ones_bd) + _mm(lo, ones_bd)

    q = y[:, :GROUP_W]
    k = y[:, GROUP_W:2 * GROUP_W]
    v = y[:, 2 * GROUP_W:]
    q = q * lax.rsqrt(head_sumsq(q * q) + EPS) * (HEAD_DIM ** -0.5)
    k = k * lax.rsqrt(head_sumsq(k * k) + EPS)
    sm = sm_ref[...]
    beta_s = jax.nn.sigmoid(sm)
    z = sm + dtb_ref[...]
    g_s = -jnp.exp(alog_ref[...]) * (jnp.maximum(z, 0.0) + jnp.log1p(jnp.exp(-jnp.abs(z))))

    er = lax.broadcasted_iota(jnp.int32, (e_len, e_len), 0)
    ec = lax.broadcasted_iota(jnp.int32, (e_len, e_len), 1)
    blk = lax.shift_right_logical(er, 6) == lax.shift_right_logical(ec, 6)
    incl = blk & ((er & (c_len - 1)) >= (ec & (c_len - 1)))
    strict = blk & ((er & (c_len - 1)) > (ec & (c_len - 1)))
    tril_bd = jnp.where(incl, 1.0, 0.0).astype(BF16)
    eye = jnp.where(er == ec, 1.0, 0.0)
    headmask = lax.shift_right_logical(er, 6) == lax.shift_right_logical(ec, 6)

    for c in range(ts // c_len):
        rows = slice(c * c_len, (c + 1) * c_len)
        qe = _expand_rows(q[rows, :], headmask)
        ke = _expand_rows(k[rows, :], headmask)
        ve = _expand_rows(v[rows, :], headmask)
        beta = _col_rows([beta_s[rows, SM_GB + h:SM_GB + h + 1] for h in range(N_HEADS)])
        g = _col_rows([g_s[rows, SM_GA + h:SM_GA + h + 1] for h in range(N_HEADS)])
        g_hi, g_mid, g_lo = _split3(g)
        gc = _mm(tril_bd, g_hi) + (_mm(tril_bd, g_mid) + _mm(tril_bd, g_lo))
        decay = jnp.exp(jnp.where(incl, gc - gc.T, -jnp.inf))
        kbeta = ke * beta
        a_mat = jnp.where(strict, _mm_nt(kbeta.astype(BF16), ke.astype(BF16)) * decay, 0.0)
        inv = eye - a_mat
        pw = a_mat
        for _ in range(5):
            pw = _mm3(pw, pw)
            inv = inv + _mm3(inv, pw)
        u = _mm3(inv, ve * beta)
        w = _mm3(inv, kbeta * jnp.exp(gc))
        attn = jnp.where(incl, _mm_nt(qe.astype(BF16), ke.astype(BF16)) * decay, 0.0)
        g_last = jnp.concatenate(
            [jnp.broadcast_to(gc[h * c_len + c_len - 1:h * c_len + c_len, :], (c_len, GROUP_W))
             for h in range(N_HEADS)], axis=0)
        s_bf = state[...].astype(BF16)
        v_new = u - _mm(w.astype(BF16), s_bf)
        o_e = _mm((qe * jnp.exp(gc)).astype(BF16), s_bf) + _mm(attn.astype(BF16), v_new.astype(BF16))
        state[...] = state[...] * jnp.exp(g_last) + _mm_tn((ke * jnp.exp(g_last - gc)).astype(BF16),
                                                           v_new.astype(BF16))
        o = o_e[0:c_len, :]
        for h in range(1, N_HEADS):
            o = o + o_e[h * c_len:(h + 1) * c_len, :]
        o = o * lax.rsqrt(head_sumsq(o * o) * (1.0 / HEAD_DIM) + EPS) * onorm_ref[...]
        gate = gate_ref[rows, :]
        o_ref[rows, :] = (o * (gate * jax.nn.sigmoid(gate))).astype(o_ref.dtype)


def _gdn(gqkv, small, gate, conv_w, a_log, dt_bias, out_norm, seq, ts=256):
    n = gqkv.shape[0]
    ts = min(ts, seq)
    hb = ts // HALO
    alog = jnp.zeros((1, LANES), F32).at[0, SM_GA:SM_GA + N_HEADS].set(a_log)
    dtb = jnp.zeros((1, LANES), F32).at[0, SM_GA:SM_GA + N_HEADS].set(dt_bias)
    onorm = jnp.tile(out_norm, N_HEADS).reshape(1, GROUP_W)
    row = lambda wd: pl.BlockSpec((ts, wd), lambda i: (i, 0))
    full = lambda a: pl.BlockSpec(a.shape, lambda i: (0, 0))
    cw = conv_w.T
    return pl.pallas_call(
        functools.partial(_gdn_kernel, tiles_per_seq=seq // ts, ts=ts),
        out_shape=jax.ShapeDtypeStruct((n, GROUP_W), BF16),
        grid=(n // ts,),
        in_specs=[row(3 * GROUP_W),
                  pl.BlockSpec((HALO, 3 * GROUP_W), lambda i: (jnp.maximum(i * hb - 1, 0), 0)),
                  row(LANES), row(GROUP_W), full(cw), full(alog), full(dtb), full(onorm)],
        out_specs=row(GROUP_W),
        scratch_shapes=[pltpu.VMEM((ts + HALO, 3 * GROUP_W), F32), pltpu.VMEM((GROUP_W, GROUP_W), F32)],
        compiler_params=_params("arbitrary"),
        name="gdn",
    )(gqkv, gqkv, small, gate, cw, alog, dtb, onorm)


def _mixout_kernel(x_ref, ya_ref, yb_ref, yc_ref, yd_ref, w_ref, o_ref):
    acc = x_ref[...]
    for g, y_ref in enumerate((ya_ref, yb_ref, yc_ref, yd_ref)):
        acc = acc + _mm(y_ref[...], w_ref[g * GROUP_W:(g + 1) * GROUP_W, :])
    o_ref[...] = acc


def _mix_out(x, ys, w, tm=512):
    n, d = x.shape
    return pl.pallas_call(
        _mixout_kernel,
        out_shape=jax.ShapeDtypeStruct((n, d), F32),
        grid=(n // tm,),
        in_specs=[pl.BlockSpec((tm, d), lambda i: (i, 0))]
                 + [pl.BlockSpec((tm, GROUP_W), lambda i: (i, 0))] * 4
                 + [pl.BlockSpec(w.shape, lambda i: (0, 0))],
        out_specs=pl.BlockSpec((tm, d), lambda i: (i, 0)),
        compiler_params=_params("parallel"),
        name="mix_out",
    )(x, *ys, w.astype(BF16))


def _xattn_kernel(x_ref, g_ref, wq_ref, k_ref, v_ref, wo_ref, o_ref, *, hd):
    x = x_ref[...]
    h = _rms(x, g_ref[...]).astype(BF16)
    q = _mm(h, wq_ref[...]).astype(BF16)
    outs = []
    for a in range(XA_HEADS):
        cols = slice(a * hd, (a + 1) * hd)
        logit = _mm_nt(q[:, cols], k_ref[:, cols]) * (hd ** -0.5)
        p = jnp.exp(logit - jnp.max(logit, axis=-1, keepdims=True))
        p = p / jnp.sum(p, axis=-1, keepdims=True)
        outs.append(_mm(p.astype(BF16), v_ref[:, cols]).astype(BF16))
    o_ref[...] = x + _mm(jnp.concatenate(outs, axis=1), wo_ref[...])


def _cross_attention(x, g, wq, k, v, wo, seq, mem_len, tm=512):
    n, d = x.shape
    tiles = seq // tm
    full = lambda a: pl.BlockSpec(a.shape, lambda i: (0, 0))
    per_seq = pl.BlockSpec((mem_len, d), lambda i: (i // tiles, 0))
    gg = g.reshape(1, d)
    wqb = wq.astype(BF16)
    wob = wo.astype(BF16)
    return pl.pallas_call(
        functools.partial(_xattn_kernel, hd=d // XA_HEADS),
        out_shape=jax.ShapeDtypeStruct((n, d), F32),
        grid=(n // tm,),
        in_specs=[pl.BlockSpec((tm, d), lambda i: (i, 0)), full(gg), full(wqb), per_seq, per_seq, full(wob)],
        out_specs=pl.BlockSpec((tm, d), lambda i: (i, 0)),
        compiler_params=_params("parallel"),
        name="cross_attention",
    )(x, gg, wqb, k, v, wob)


def _rope_tables(positions):
    half = ROT_DIM // 2
    inv = ROPE_THETA ** (-jnp.arange(half, dtype=F32) * (2.0 / ROT_DIM))
    ang = positions.reshape(-1).astype(F32)[:, None] * inv
    cos, sin = jnp.cos(ang), jnp.sin(ang)
    n = ang.shape[0]
    rest = HEAD_DIM - ROT_DIM
    cos64 = jnp.concatenate([cos, cos, jnp.ones((n, rest), F32)], axis=1)
    sin64 = jnp.concatenate([-sin, sin, jnp.zeros((n, rest), F32)], axis=1)
    return jnp.tile(cos64, (1, 2)), jnp.tile(sin64, (1, 2))


def _mix_in_weight(w):
    sizes = (2 * GROUP_W, 3 * GROUP_W, GROUP_W, KV_LATENT, N_HEADS * IDX_DIM, IDX_DIM, N_HEADS,
             3 * GROUP_W, N_HEADS, N_HEADS, GROUP_W)
    offs = np.concatenate([[0], np.cumsum(sizes)])
    p = [w[:, offs[i]:offs[i + 1]] for i in range(len(sizes))]
    sg, sc, q, ckv, iq, ik, iw, gqkv, ga, gb, gg = p
    pad = jnp.zeros((w.shape[0], LANES - IDX_DIM - 3 * N_HEADS), w.dtype)
    small = jnp.concatenate([ik, iw, ga, gb, pad], axis=1)
    return jnp.concatenate([sg, sc, q, ckv, iq, gqkv, gg, small], axis=1)


MIX_WIDTHS = (2 * GROUP_W, 3 * GROUP_W, GROUP_W, KV_LATENT, GROUP_W, 3 * GROUP_W, GROUP_W, LANES)


def kernel(x, mem, positions, ffn1_norm, ffn1_wg, ffn1_wu, ffn1_wd, mix_norm, w_mix_in, w_mix_out, sg_norm, sg_w, sg_b, sc_conv, kv_norm, w_uk, w_uv, idx_k_norm, gdn_conv, gdn_a_log, gdn_dt_bias, gdn_out_norm, xa_norm, mem_norm, xa_wq, xa_wk, xa_wv, xa_wo, ffn2_norm, ffn2_wg, ffn2_wu, ffn2_wd, final_norm):
    bsz, seq, d = x.shape
    mem_len = mem.shape[1]
    depth = ffn1_wg.shape[0]
    n = bsz * seq
    x = x.reshape(n, d)
    mem2 = mem.reshape(bsz * mem_len, d)
    cos, sin = _rope_tables(positions)
    for l in range(depth):
        x = _ffn(x, ffn1_norm[l], ffn1_wg[l], ffn1_wu[l], ffn1_wd[l])
        p_sg, p_sc, p_q, p_ckv, p_iq, p_gqkv, p_gg, p_small = _norm_proj(
            x, mix_norm[l], _mix_in_weight(w_mix_in[l]), MIX_WIDTHS, (F32,) * len(MIX_WIDTHS))
        y_a = _spatial_gating(p_sg, sg_norm[l], sg_w[l], sg_b[l])
        y_b = _short_conv(p_sc, sc_conv[l], seq)
        q_r, k_r, v_r, iq_r, ik_r = _dsa_prep(p_q, p_ckv, p_iq, p_small, cos, sin,
                                              kv_norm[l], w_uk[l], w_uv[l], idx_k_norm[l])
        y_c = _dsa(q_r, iq_r, p_small, k_r, v_r, ik_r, seq)
        y_d = _gdn(p_gqkv, p_small, p_gg, gdn_conv[l], gdn_a_log[l], gdn_dt_bias[l], gdn_out_norm[l], seq)
        x = _mix_out(x, (y_a, y_b, y_c, y_d), w_mix_out[l])
        mk, mv = _norm_proj(mem2, mem_norm[l], jnp.concatenate([xa_wk[l], xa_wv[l]], axis=1),
                            (d, d), (BF16, BF16))
        x = _cross_attention(x, xa_norm[l], xa_wq[l], mk, mv, xa_wo[l], seq, mem_len)
        x = _ffn(x, ffn2_norm[l], ffn2_wg[l], ffn2_wu[l], ffn2_wd[l],
                 final_g=final_norm if l == depth - 1 else None)
    return x.reshape(bsz, seq, d)
```
